```python
import math
import jax, jax.numpy as jnp
from jax import lax
import numpy as np

D_MODEL = 1024
BATCH = 2
SEQ = 16384
DEPTH = 2

CHUNK = 64
N_MIXERS = 2
EXPAND = 2
D_INNER = EXPAND * D_MODEL
SC_WIDTH = 3
SSD_HEAD_DIM = 64
SSD_HEADS = D_INNER // SSD_HEAD_DIM
SSD_GROUPS = 8
SSD_HPG = SSD_HEADS // SSD_GROUPS
SSD_STATE = 128
SSD_CONV = 4
SSD_XBC = D_INNER + 2 * SSD_GROUPS * SSD_STATE
SSD_IN = D_INNER + SSD_XBC + SSD_HEADS
N_SC_LAYERS = (DEPTH + N_MIXERS - 1) // N_MIXERS
N_SSD_LAYERS = DEPTH // N_MIXERS
EPS = 1e-6

kernel_name = "hybrid_shortconv_ssd_sandwich"


def rms_norm(x, g):
    xf = x.astype(jnp.float32)
    y = xf * lax.rsqrt(jnp.mean(xf * xf, axis=-1, keepdims=True) + EPS)
    return (y * g.astype(jnp.float32)).astype(x.dtype)


def causal_dwconv(u, w):
    width, ch = w.shape
    return lax.conv_general_dilated(
        u, w[:, None, :].astype(u.dtype), window_strides=(1,),
        padding=[(width - 1, 0)], dimension_numbers=("NWC", "WIO", "NWC"),
        feature_group_count=ch)


def short_conv_mixer(u, w_in, conv_w, w_out):
    z, bg, cg, v = jnp.split(u @ w_in.astype(u.dtype), 4, axis=-1)
    y = bg * causal_dwconv(cg * v, conv_w) * jax.nn.silu(z)
    return y @ w_out.astype(u.dtype)


def ssd_chunked(x, dt, a, bm, cm):
    bsz, seqlen = x.shape[0], x.shape[1]
    nc = seqlen // CHUNK
    x = x.astype(jnp.float32).reshape(bsz, nc, CHUNK, SSD_GROUPS, SSD_HPG, SSD_HEAD_DIM)
    dt = dt.reshape(bsz, nc, CHUNK, SSD_GROUPS, SSD_HPG)
    a = a.reshape(SSD_GROUPS, SSD_HPG)
    bm = bm.astype(jnp.float32).reshape(bsz, nc, CHUNK, SSD_GROUPS, SSD_STATE)
    cm = cm.astype(jnp.float32).reshape(bsz, nc, CHUNK, SSD_GROUPS, SSD_STATE)

    a_cum = jnp.cumsum(dt * a, axis=2)
    xdt = x * dt[..., None]

    ac = jnp.moveaxis(a_cum, 2, -1)
    seg = ac[..., :, None] - ac[..., None, :]
    mask = jnp.tril(jnp.ones((CHUNK, CHUNK), dtype=bool))
    decay = jnp.exp(jnp.where(mask, seg, -jnp.inf))
    cb = jnp.einsum("bclgn,bcsgn->bcgls", cm, bm)
    y_diag = jnp.einsum("bcgkls,bcsgkp->bclgkp", cb[:, :, :, None] * decay, xdt)

    decay_states = jnp.exp(a_cum[:, :, -1:] - a_cum)
    states = jnp.einsum("bclgn,bclgk,bclgkp->bcgkpn", bm, decay_states, xdt)
    chunk_decay = jnp.exp(a_cum[:, :, -1])

    def step(h, inp):
        s, d = inp
        return h * d[..., None, None] + s, h

    h0 = jnp.zeros((bsz, SSD_GROUPS, SSD_HPG, SSD_HEAD_DIM, SSD_STATE), jnp.float32)
    _, h_prev = lax.scan(step, h0, (jnp.moveaxis(states, 1, 0), jnp.moveaxis(chunk_decay, 1, 0)))
    h_prev = jnp.moveaxis(h_prev, 0, 1)

    y_off = jnp.einsum("bclgn,bcgkpn,bclgk->bclgkp", cm, h_prev, jnp.exp(a_cum))
    return (y_diag + y_off).reshape(bsz, seqlen, SSD_HEADS, SSD_HEAD_DIM)


def ssd_mixer(u, w_in, conv_w, conv_b, dt_bias, a_log, d_skip, norm_w, w_out):
    bsz, seqlen, _ = u.shape
    proj = u @ w_in.astype(u.dtype)
    z = proj[..., :D_INNER]
    xbc = proj[..., D_INNER:D_INNER + SSD_XBC]
    dt_raw = proj[..., D_INNER + SSD_XBC:]
    xbc = jax.nn.silu(causal_dwconv(xbc, conv_w) + conv_b.astype(u.dtype))
    gn = SSD_GROUPS * SSD_STATE
    xs = xbc[..., :D_INNER]
    bm = xbc[..., D_INNER:D_INNER + gn]
    cm = xbc[..., D_INNER + gn:]
    dt = jax.nn.softplus(dt_raw.astype(jnp.float32) + dt_bias.astype(jnp.float32))
    a = -jnp.exp(a_log.astype(jnp.float32))
    y = ssd_chunked(xs, dt, a, bm, cm)
    y = y + d_skip.astype(jnp.float32)[:, None] * xs.astype(jnp.float32).reshape(
        bsz, seqlen, SSD_HEADS, SSD_HEAD_DIM)
    yg = (y.reshape(bsz, seqlen, D_INNER) * jax.nn.silu(z.astype(jnp.float32)))
    yg = yg.reshape(bsz, seqlen, SSD_GROUPS, D_INNER // SSD_GROUPS)
    yg = yg * lax.rsqrt(jnp.mean(yg * yg, axis=-1, keepdims=True) + EPS)
    yg = yg.reshape(bsz, seqlen, D_INNER) * norm_w.astype(jnp.float32)
    return yg.astype(u.dtype) @ w_out.astype(u.dtype)


def setup_inputs(seed: int = 0) -> dict:
    key = jax.random.key(seed)
    ks = jax.random.split(key, 16)
    f32 = jnp.float32
    x = jax.random.normal(ks[0], (BATCH, SEQ, D_MODEL), f32)
    pre_norm = 1.0 + 0.02 * jax.random.normal(ks[1], (DEPTH, D_MODEL), f32)
    post_norm = 1.0 + 0.02 * jax.random.normal(ks[2], (DEPTH, D_MODEL), f32)
    sc_w_in = jax.random.normal(ks[3], (N_SC_LAYERS, D_MODEL, 4 * D_INNER), f32) * D_MODEL ** -0.5
    sc_conv_w = jax.random.normal(ks[4], (N_SC_LAYERS, SC_WIDTH, D_INNER), f32) * SC_WIDTH ** -0.5
    sc_w_out = jax.random.normal(ks[5], (N_SC_LAYERS, D_INNER, D_MODEL), f32) * D_INNER ** -0.5
    ssd_w_in = jax.random.normal(ks[6], (N_SSD_LAYERS, D_MODEL, SSD_IN), f32) * D_MODEL ** -0.5
    ssd_conv_w = jax.random.normal(ks[7], (N_SSD_LAYERS, SSD_CONV, SSD_XBC), f32) * SSD_CONV ** -0.5
    ssd_conv_b = 0.01 * jax.random.normal(ks[8], (N_SSD_LAYERS, SSD_XBC), f32)
    dt0 = jnp.exp(jax.random.uniform(ks[9], (N_SSD_LAYERS, SSD_HEADS), f32,
                                     math.log(1e-3), math.log(1e-1)))
    ssd_dt_bias = dt0 + jnp.log(-jnp.expm1(-dt0))
    ssd_a_log = jnp.log(jax.random.uniform(ks[10], (N_SSD_LAYERS, SSD_HEADS), f32, 1.0, 16.0))
    ssd_d_skip = 1.0 + 0.1 * jax.random.normal(ks[11], (N_SSD_LAYERS, SSD_HEADS), f32)
    ssd_norm = 1.0 + 0.02 * jax.random.normal(ks[12], (N_SSD_LAYERS, D_INNER), f32)
    ssd_w_out = jax.random.normal(ks[13], (N_SSD_LAYERS, D_INNER, D_MODEL), f32) * D_INNER ** -0.5
    return {"x": x, "pre_norm": pre_norm, "post_norm": post_norm,
            "sc_w_in": sc_w_in, "sc_conv_w": sc_conv_w, "sc_w_out": sc_w_out,
            "ssd_w_in": ssd_w_in, "ssd_conv_w": ssd_conv_w, "ssd_conv_b": ssd_conv_b,
            "ssd_dt_bias": ssd_dt_bias, "ssd_a_log": ssd_a_log, "ssd_d_skip": ssd_d_skip,
            "ssd_norm": ssd_norm, "ssd_w_out": ssd_w_out}


def reference(x, pre_norm, post_norm, sc_w_in, sc_conv_w, sc_w_out, ssd_w_in, ssd_conv_w,
              ssd_conv_b, ssd_dt_bias, ssd_a_log, ssd_d_skip, ssd_norm, ssd_w_out):
    h = x
    for i in range(DEPTH):
        u = rms_norm(h, pre_norm[i])
        j = i // N_MIXERS
        if i % N_MIXERS == 0:
            m = short_conv_mixer(u, sc_w_in[j], sc_conv_w[j], sc_w_out[j])
        else:
            m = ssd_mixer(u, ssd_w_in[j], ssd_conv_w[j], ssd_conv_b[j], ssd_dt_bias[j],
                          ssd_a_log[j], ssd_d_skip[j], ssd_norm[j], ssd_w_out[j])
        h = h + rms_norm(m, post_norm[i])
    return h
```

```python
import functools

import jax
import jax.numpy as jnp
from jax import lax
from jax.experimental import pallas as pl
from jax.experimental.pallas import tpu as pltpu

D_MODEL = 1024
D_INNER = 2048
N_MIXERS = 2
SSD_HEAD_DIM = 64
SSD_HEADS = D_INNER // SSD_HEAD_DIM
SSD_GROUPS = 8
SSD_HPG = SSD_HEADS // SSD_GROUPS
SSD_STATE = 128
SSD_GN = SSD_GROUPS * SSD_STATE
SSD_XBC = D_INNER + 2 * SSD_GN
GROUP_WIDTH = D_INNER // SSD_GROUPS
EPS = 1e-6

LANES = 128
SUBLANES = 8
VMEM_LIMIT_BYTES = 52 * 1024 * 1024

SC_TOKEN_TILE = 512
SC_COL_TILE = 512
SSD_TOKEN_TILE = 256
SSD_CHUNK = 128
CONV_COL_TILE = 512

F32 = jnp.float32
BF16 = jnp.bfloat16


def _rms(x, g):
    return x * lax.rsqrt(jnp.mean(x * x, axis=-1, keepdims=True) + EPS) * g


def _silu(x):
    return x * jax.nn.sigmoid(x)


def _dot(a, b):
    return jnp.dot(a, b, preferred_element_type=F32)


def _split3(v):
    hi = v.astype(BF16).astype(F32)
    r = v - hi
    mid = r.astype(BF16).astype(F32)
    lo = (r - mid).astype(BF16).astype(F32)
    return hi, mid, lo


def _sc_layer_kernel(h_ref, pre_g_ref, w_in_ref, conv_w_ref, w_out_ref, post_g_ref,
                     out_ref, halo_ref, y_ref, *, tm, tn, tiles_per_seq):
    i = pl.program_id(0)

    @pl.when(i % tiles_per_seq == 0)
    def _():
        halo_ref[...] = jnp.zeros_like(halo_ref)

    x = h_ref[...]
    u = _rms(x, pre_g_ref[...]).astype(BF16)
    for j in range(D_INNER // tn):
        c0 = j * tn

        def proj(k):
            return _dot(u, w_in_ref[:, k * D_INNER + c0:k * D_INNER + c0 + tn])

        cv = proj(2) * proj(3)
        ext = jnp.concatenate([halo_ref[:, c0:c0 + tn], cv], axis=0)
        halo_ref[:, c0:c0 + tn] = cv[tm - SUBLANES:, :]
        cw = conv_w_ref[:, c0:c0 + tn]
        conv = (cw[2:3] * cv
                + cw[1:2] * pltpu.roll(ext, 1, 0)[SUBLANES:]
                + cw[0:1] * pltpu.roll(ext, 2, 0)[SUBLANES:])
        y = proj(1) * conv * _silu(proj(0))
        y_ref[:, c0:c0 + tn] = y.astype(BF16)
    m = _dot(y_ref[...], w_out_ref[...])
    out_ref[...] = x + _rms(m, post_g_ref[...])


def _sc_layer(h, pre_g, post_g, w_in, conv_w, w_out, *, seq_len):
    t = h.shape[0]
    tm, tn = SC_TOKEN_TILE, SC_COL_TILE
    assert t % tm == 0 and seq_len % tm == 0 and D_INNER % tn == 0
    const = lambda i: (0, 0)
    resident = functools.partial(pl.BlockSpec, index_map=const, pipeline_mode=pl.Buffered(1))
    return pl.pallas_call(
        functools.partial(_sc_layer_kernel, tm=tm, tn=tn, tiles_per_seq=seq_len // tm),
        grid=(t // tm,),
        in_specs=[
            pl.BlockSpec((tm, D_MODEL), lambda i: (i, 0)),
            resident((1, D_MODEL)),
            resident((D_MODEL, 4 * D_INNER)),
            resident(conv_w.shape),
            resident((D_INNER, D_MODEL)),
            resident((1, D_MODEL)),
        ],
        out_specs=pl.BlockSpec((tm, D_MODEL), lambda i: (i, 0)),
        out_shape=jax.ShapeDtypeStruct((t, D_MODEL), F32),
        scratch_shapes=[
            pltpu.VMEM((SUBLANES, D_INNER), F32),
            pltpu.VMEM((tm, D_INNER), BF16),
        ],
        compiler_params=pltpu.CompilerParams(
            dimension_semantics=("arbitrary",), vmem_limit_bytes=VMEM_LIMIT_BYTES),
        name="short_conv_layer",
    )(h, pre_g[None], w_in.astype(BF16), conv_w, w_out.astype(BF16), post_g[None])


def _ssd_layer_kernel(h_ref, pre_g_ref, w_main_ref, w_dt_ref, conv_w_ref, conv_b_ref,
                      dt_bias_ref, a_log_ref, dskip_ref, norm_w_ref, w_out_ref, post_g_ref,
                      expand_ref, out_ref, proj_ref, state_ref, y_ref,
                      *, tm, q, tiles_per_seq):
    i = pl.program_id(0)
    halo = SUBLANES

    @pl.when(i % tiles_per_seq == 0)
    def _():
        state_ref[...] = jnp.zeros_like(state_ref)
        proj_ref[0:halo, :] = jnp.zeros((halo, proj_ref.shape[1]), F32)

    x = h_ref[...]
    u = _rms(x, pre_g_ref[...]).astype(BF16)
    proj_ref[halo:, :] = _dot(u, w_main_ref[...])
    lane = lax.broadcasted_iota(jnp.int32, (1, LANES), 1)
    dt = jnp.where(lane < SSD_HEADS,
                   jax.nn.softplus(_dot(u, w_dt_ref[...]) + dt_bias_ref[...]), 0.0)
    a = -jnp.exp(a_log_ref[...])

    cwid = CONV_COL_TILE
    for c in range(SSD_XBC // cwid):
        cols = slice(D_INNER + c * cwid, D_INNER + (c + 1) * cwid)
        ext = proj_ref[:, cols]
        w = conv_w_ref[:, c * cwid:(c + 1) * cwid]
        acc = (conv_b_ref[:, c * cwid:(c + 1) * cwid]
               + w[3:4] * ext[halo:]
               + w[2:3] * pltpu.roll(ext, 1, 0)[halo:]
               + w[1:2] * pltpu.roll(ext, 2, 0)[halo:]
               + w[0:1] * pltpu.roll(ext, 3, 0)[halo:])
        proj_ref[0:halo, cols] = ext[tm:tm + halo]
        proj_ref[halo:, cols] = _silu(acc)

    row_id = lax.broadcasted_iota(jnp.int32, (q, q), 0)
    col_id = lax.broadcasted_iota(jnp.int32, (q, q), 1)
    tril = row_id >= col_id
    tril_bf = tril.astype(BF16)
    bd_row = lax.broadcasted_iota(jnp.int32, (SSD_HPG * q, GROUP_WIDTH), 0) // q
    bd_col = lax.broadcasted_iota(jnp.int32, (SSD_HPG * q, GROUP_WIDTH), 1) // SSD_HEAD_DIM
    block_diag = bd_row == bd_col

    def pack3(v):
        hi, mid, lo = _split3(jnp.where(lane < SSD_HEADS, v, 0.0))
        return (hi + pltpu.roll(mid, SSD_HEADS, 1) + pltpu.roll(lo, 2 * SSD_HEADS, 1)).astype(BF16)

    for c in range(tm // q):
        rows = slice(halo + c * q, halo + (c + 1) * q)
        dtc = dt[c * q:(c + 1) * q]
        hi, mid, lo = _split3(dtc * a)
        cs = _dot(tril_bf, jnp.concatenate([hi, mid, lo], axis=1).astype(BF16))
        acum = cs[:, :LANES] + cs[:, LANES:2 * LANES] + cs[:, 2 * LANES:]
        acum_t = acum.T
        alast = acum[q - 1:q, :]
        w1 = dtc * jnp.exp(alast - acum)
        ea = jnp.exp(acum)
        packed = jnp.concatenate([pack3(dtc), pack3(w1), pack3(ea)], axis=0)
        for g in range(SSD_GROUPS):
            gcols = slice(g * GROUP_WIDTH, (g + 1) * GROUP_WIDTH)
            xg = proj_ref[rows, D_INNER + g * GROUP_WIDTH:D_INNER + (g + 1) * GROUP_WIDTH]
            bg = proj_ref[rows, 2 * D_INNER + g * SSD_STATE:2 * D_INNER + (g + 1) * SSD_STATE]
            cg = proj_ref[rows, 2 * D_INNER + SSD_GN + g * SSD_STATE:
                          2 * D_INNER + SSD_GN + (g + 1) * SSD_STATE]
            exg = _dot(packed, expand_ref[:, gcols])
            dtx, w1x, eax = exg[0:q], exg[q:2 * q], exg[2 * q:3 * q]
            xdt = (xg * dtx).astype(BF16)
            xw = (xg * w1x).astype(BF16)
            bb = bg.astype(BF16)
            cb16 = cg.astype(BF16)
            cb = lax.dot_general(cb16, bb, (((1,), (1,)), ((), ())), preferred_element_type=F32)
            lmats = []
            for k in range(SSD_HPG):
                hh = g * SSD_HPG + k
                seg = acum[:, hh:hh + 1] - acum_t[hh:hh + 1, :]
                lmats.append((cb * jnp.exp(jnp.where(tril, seg, -jnp.inf))).astype(BF16))
            lcat = jnp.concatenate(lmats, axis=1)
            xbd = jnp.where(block_diag, jnp.concatenate([xdt] * SSD_HPG, axis=0), 0)
            y_diag = _dot(lcat, xbd)
            hg = state_ref[g]
            y_off = _dot(cb16, hg.astype(BF16)) * eax
            st = _dot(bg.T.astype(BF16), xw)
            state_ref[g] = hg * eax[q - 1:q, :] + st
            y = y_diag + y_off + dskip_ref[:, gcols] * xg
            yg = y * _silu(proj_ref[rows, gcols])
            yn = yg * lax.rsqrt(jnp.mean(yg * yg, axis=-1, keepdims=True) + EPS) * norm_w_ref[:, gcols]
            y_ref[c * q:(c + 1) * q, gcols] = yn.astype(BF16)

    m = _dot(y_ref[...], w_out_ref[...])
    out_ref[...] = x + _rms(m, post_g_ref[...])


def _ssd_layer(h, pre_g, post_g, w_in, conv_w, conv_b, dt_bias, a_log, d_skip, norm_w, w_out,
               *, seq_len):
    t = h.shape[0]
    tm, q = SSD_TOKEN_TILE, SSD_CHUNK
    assert t % tm == 0 and seq_len % tm == 0 and tm % q == 0
    n_main = D_INNER + SSD_XBC
    w_main = w_in[:, :n_main].astype(BF16)
    pad_heads = lambda v: jnp.pad(v, (0, LANES - SSD_HEADS))[None]
    w_dt = jnp.pad(w_in[:, n_main:], ((0, 0), (0, LANES - SSD_HEADS))).astype(BF16)
    piece_head = jnp.arange(LANES) % SSD_HEADS
    valid = jnp.arange(LANES) < 3 * SSD_HEADS
    chan_head = jnp.arange(D_INNER) // SSD_HEAD_DIM
    expand = ((piece_head[:, None] == chan_head[None, :]) & valid[:, None]).astype(BF16)

    const = lambda i: (0, 0)
    resident = functools.partial(pl.BlockSpec, index_map=const, pipeline_mode=pl.Buffered(1))
    return pl.pallas_call(
        functools.partial(_ssd_layer_kernel, tm=tm, q=q, tiles_per_seq=seq_len // tm),
        grid=(t // tm,),
        in_specs=[
            pl.BlockSpec((tm, D_MODEL), lambda i: (i, 0)),
            resident((1, D_MODEL)),
            resident((D_MODEL, n_main)),
            resident((D_MODEL, LANES)),
            resident(conv_w.shape),
            resident((1, SSD_XBC)),
            resident((1, LANES)),
            resident((1, LANES)),
            resident((1, D_INNER)),
            resident((1, D_INNER)),
            resident((D_INNER, D_MODEL)),
            resident((1, D_MODEL)),
            resident((LANES, D_INNER)),
        ],
        out_specs=pl.BlockSpec((tm, D_MODEL), lambda i: (i, 0)),
        out_shape=jax.ShapeDtypeStruct((t, D_MODEL), F32),
        scratch_shapes=[
            pltpu.VMEM((SUBLANES + tm, n_main), F32),
            pltpu.VMEM((SSD_GROUPS, SSD_STATE, GROUP_WIDTH), F32),
            pltpu.VMEM((tm, D_INNER), BF16),
        ],
        compiler_params=pltpu.CompilerParams(
            dimension_semantics=("arbitrary",), vmem_limit_bytes=VMEM_LIMIT_BYTES),
        name="ssd_layer",
    )(h, pre_g[None], w_main, w_dt, conv_w, conv_b[None], pad_heads(dt_bias), pad_heads(a_log),
      jnp.repeat(d_skip, SSD_HEAD_DIM)[None], norm_w[None], w_out.astype(BF16), post_g[None],
      expand)


def kernel(x, pre_norm, post_norm, sc_w_in, sc_conv_w, sc_w_out, ssd_w_in, ssd_conv_w, ssd_conv_b,
           ssd_dt_bias, ssd_a_log, ssd_d_skip, ssd_norm, ssd_w_out):
    bsz, seq_len, _ = x.shape
    h = x.reshape(bsz * seq_len, D_MODEL)
    for i in range(pre_norm.shape[0]):
        j = i // N_MIXERS
        if i % N_MIXERS == 0:
            h = _sc_layer(h, pre_norm[i], post_norm[i], sc_w_in[j], sc_conv_w[j], sc_w_out[j],
                          seq_len=seq_len)
        else:
            h = _ssd_layer(h, pre_norm[i], post_norm[i], ssd_w_in[j], ssd_conv_w[j], ssd_conv_b[j],
                           ssd_dt_bias[j], ssd_a_log[j], ssd_d_skip[j], ssd_norm[j], ssd_w_out[j],
                           seq_len=seq_len)
    return h.reshape(bsz, seq_len, D_MODEL)
```

```python
import functools

import jax
import jax.numpy as jnp
from jax import lax
from jax.experimental import pallas as pl
from jax.experimental.pallas import tpu as pltpu

D_MODEL = 1024
D_INNER = 2048
N_MIXERS = 2
SSD_HEAD_DIM = 64
SSD_HEADS = D_INNER // SSD_HEAD_DIM
SSD_GROUPS = 8
SSD_HPG = SSD_HEADS // SSD_GROUPS
SSD_STATE = 128
SSD_GN = SSD_GROUPS * SSD_STATE
SSD_XBC = D_INNER + 2 * SSD_GN
SSD_CONV = 4
GROUP_WIDTH = D_INNER // SSD_GROUPS
GROUP_XBC = GROUP_WIDTH + 2 * SSD_STATE
GROUP_PROJ = GROUP_WIDTH + GROUP_XBC
EPS = 1e-6

LANES = 128
SUBLANES = 8
VMEM_LIMIT_BYTES = 52 * 1024 * 1024

SC_TOKEN_TILE = 512
SC_COL_TILE = 512
SSD_TOKEN_TILE = 256
SSD_CHUNK = 128
OUT_COL_BLOCK = 512

F32 = jnp.float32
BF16 = jnp.bfloat16


def _rms(x, g):
    return x * lax.rsqrt(jnp.mean(x * x, axis=-1, keepdims=True) + EPS) * g


def _silu(x):
    return x * jax.nn.sigmoid(x)


def _dot(a, b):
    return jnp.dot(a, b, preferred_element_type=F32)


def _col_blocks(w, width):
    k, n = w.shape
    return w.reshape(k, n // width, width).transpose(1, 0, 2).astype(BF16)


def _out_proj(y, w_out_ref):
    return jnp.concatenate([_dot(y, w_out_ref[b]) for b in range(w_out_ref.shape[0])], axis=1)


def _resident_specs():
    mk = lambda nd: functools.partial(pl.BlockSpec, index_map=lambda i: (0,) * nd,
                                      pipeline_mode=pl.Buffered(1))
    return mk(2), mk(3)


def _split3(v):
    hi = v.astype(BF16).astype(F32)
    r = v - hi
    mid = r.astype(BF16).astype(F32)
    lo = (r - mid).astype(BF16).astype(F32)
    return hi, mid, lo


def _sc_layer_kernel(h_ref, pre_g_ref, w_in_ref, conv_w_ref, w_out_ref, post_g_ref,
                     out_ref, halo_ref, y_ref, *, tm, tn, tiles_per_seq):
    i = pl.program_id(0)

    @pl.when(i % tiles_per_seq == 0)
    def _():
        halo_ref[...] = jnp.zeros_like(halo_ref)

    x = h_ref[...]
    u = _rms(x, pre_g_ref[...]).astype(BF16)
    for j in range(D_INNER // tn):
        c0 = j * tn

        def proj(k):
            return _dot(u, w_in_ref[k * (D_INNER // tn) + j])

        cv = proj(2) * proj(3)
        ext = jnp.concatenate([halo_ref[:, c0:c0 + tn], cv], axis=0)
        halo_ref[:, c0:c0 + tn] = cv[tm - SUBLANES:, :]
        cw = conv_w_ref[:, c0:c0 + tn]
        conv = (cw[2:3] * cv
                + cw[1:2] * pltpu.roll(ext, 1, 0)[SUBLANES:]
                + cw[0:1] * pltpu.roll(ext, 2, 0)[SUBLANES:])
        y = proj(1) * conv * _silu(proj(0))
        y_ref[:, c0:c0 + tn] = y.astype(BF16)
    out_ref[...] = x + _rms(_out_proj(y_ref[...], w_out_ref), post_g_ref[...])


def _sc_layer(h, pre_g, post_g, w_in, conv_w, w_out, *, seq_len):
    t = h.shape[0]
    tm, tn = SC_TOKEN_TILE, SC_COL_TILE
    assert t % tm == 0 and seq_len % tm == 0 and D_INNER % tn == 0
    resident, resident3 = _resident_specs()
    return pl.pallas_call(
        functools.partial(_sc_layer_kernel, tm=tm, tn=tn, tiles_per_seq=seq_len // tm),
        grid=(t // tm,),
        in_specs=[
            pl.BlockSpec((tm, D_MODEL), lambda i: (i, 0)),
            resident((1, D_MODEL)),
            resident3((4 * D_INNER // tn, D_MODEL, tn)),
            resident(conv_w.shape),
            resident3((D_MODEL // OUT_COL_BLOCK, D_INNER, OUT_COL_BLOCK)),
            resident((1, D_MODEL)),
        ],
        out_specs=pl.BlockSpec((tm, D_MODEL), lambda i: (i, 0)),
        out_shape=jax.ShapeDtypeStruct((t, D_MODEL), F32),
        scratch_shapes=[
            pltpu.VMEM((SUBLANES, D_INNER), F32),
            pltpu.VMEM((tm, D_INNER), BF16),
        ],
        compiler_params=pltpu.CompilerParams(
            dimension_semantics=("arbitrary",), vmem_limit_bytes=VMEM_LIMIT_BYTES),
        name="short_conv_layer",
    )(h, pre_g[None], _col_blocks(w_in, tn), conv_w, _col_blocks(w_out, OUT_COL_BLOCK), post_g[None])


def _ssd_layer_kernel(h_ref, pre_g_ref, w_main_ref, w_dt_ref, conv_w_ref, conv_b_ref,
                      dt_bias_ref, a_log_ref, dskip_ref, norm_w_ref, w_out_ref, post_g_ref,
                      expand_ref, out_ref, *scratch, tm, q, tiles_per_seq):
    proj_refs = scratch[:SSD_GROUPS]
    xbc_refs = scratch[SSD_GROUPS:2 * SSD_GROUPS]
    state_ref, y_ref = scratch[2 * SSD_GROUPS:]
    i = pl.program_id(0)
    halo = SUBLANES
    n_chunks = tm // q

    @pl.when(i % tiles_per_seq == 0)
    def _():
        state_ref[...] = jnp.zeros_like(state_ref)
        for proj_ref in proj_refs:
            proj_ref[0:halo, :] = jnp.zeros((halo, GROUP_PROJ), F32)

    x = h_ref[...]
    u = _rms(x, pre_g_ref[...]).astype(BF16)
    lane = lax.broadcasted_iota(jnp.int32, (1, LANES), 1)
    head_lane = lane < SSD_HEADS
    dt = jnp.where(head_lane, jax.nn.softplus(_dot(u, w_dt_ref[...]) + dt_bias_ref[...]), 0.0)
    a = -jnp.exp(a_log_ref[...])

    row_id = lax.broadcasted_iota(jnp.int32, (q, q), 0)
    col_id = lax.broadcasted_iota(jnp.int32, (q, q), 1)
    tril = row_id >= col_id
    tril_bf = tril.astype(BF16)
    bd_row = lax.broadcasted_iota(jnp.int32, (SSD_HPG * q, GROUP_WIDTH), 0) // q
    bd_col = lax.broadcasted_iota(jnp.int32, (SSD_HPG * q, GROUP_WIDTH), 1) // SSD_HEAD_DIM
    block_diag = bd_row == bd_col

    def pack3(v):
        hi, mid, lo = _split3(jnp.where(head_lane, v, 0.0))
        return (hi + pltpu.roll(mid, SSD_HEADS, 1) + pltpu.roll(lo, 2 * SSD_HEADS, 1)).astype(BF16)

    acums, acum_ts, packs = [], [], []
    for c in range(n_chunks):
        dtc = dt[c * q:(c + 1) * q]
        hi, mid, lo = _split3(dtc * a)
        cs = _dot(tril_bf, jnp.concatenate([hi, mid, lo], axis=1).astype(BF16))
        acum = cs[:, :LANES] + cs[:, LANES:2 * LANES] + cs[:, 2 * LANES:]
        w1 = dtc * jnp.exp(acum[q - 1:q, :] - acum)
        acums.append(acum)
        acum_ts.append(acum.T)
        packs += [pack3(dtc), pack3(w1), pack3(jnp.exp(acum))]
    packed = jnp.concatenate(packs, axis=0)

    def project(g):
        proj_refs[g][halo:, :] = _dot(u, w_main_ref[g])

    def conv(g):
        proj_ref = proj_refs[g]
        xcols = slice(GROUP_WIDTH, GROUP_PROJ)
        ccols = slice(g * GROUP_XBC, (g + 1) * GROUP_XBC)
        w = conv_w_ref[:, ccols]
        acc = conv_b_ref[:, ccols]
        for k in range(SSD_CONV):
            acc = acc + w[SSD_CONV - 1 - k:SSD_CONV - k] * proj_ref[pl.ds(halo - k, tm), xcols]
        proj_ref[0:halo, xcols] = proj_ref[tm:tm + halo, xcols]
        xbc_refs[g][...] = _silu(acc)

    def expand_heads(g):
        return _dot(packed, expand_ref[:, g * GROUP_WIDTH:(g + 1) * GROUP_WIDTH])

    def stage_cb(g, c, exg):
        rows = slice(c * q, (c + 1) * q)
        xbc_ref = xbc_refs[g]
        xg = xbc_ref[rows, 0:GROUP_WIDTH]
        bg = xbc_ref[rows, GROUP_WIDTH:GROUP_WIDTH + SSD_STATE]
        cb16 = xbc_ref[rows, GROUP_WIDTH + SSD_STATE:GROUP_XBC].astype(BF16)
        e0 = c * 3 * q
        dtx, w1x, eax = exg[e0:e0 + q], exg[e0 + q:e0 + 2 * q], exg[e0 + 2 * q:e0 + 3 * q]
        cb = lax.dot_general(cb16, bg.astype(BF16), (((1,), (1,)), ((), ())),
                             preferred_element_type=F32)
        return dict(cb=cb, cb16=cb16, bt=bg.T.astype(BF16), xdt=(xg * dtx).astype(BF16),
                    xw=(xg * w1x).astype(BF16), eax=eax)

    def stage_state(g, c, v):
        hg = state_ref[g]
        v["y_off"] = _dot(v["cb16"], hg.astype(BF16)) * v["eax"]
        state_ref[g] = hg * v["eax"][q - 1:q, :] + _dot(v["bt"], v["xw"])

    def stage_out(g, c, v):
        rows = slice(c * q, (c + 1) * q)
        gcols = slice(g * GROUP_WIDTH, (g + 1) * GROUP_WIDTH)
        lmats = []
        for k in range(SSD_HPG):
            hh = g * SSD_HPG + k
            seg = acums[c][:, hh:hh + 1] - acum_ts[c][hh:hh + 1, :]
            lmats.append((v["cb"] * jnp.exp(jnp.where(tril, seg, -jnp.inf))).astype(BF16))
        lcat = jnp.concatenate(lmats, axis=1)
        xbd = jnp.where(block_diag, jnp.concatenate([v["xdt"]] * SSD_HPG, axis=0), 0)
        y = _dot(lcat, xbd) + v["y_off"] + dskip_ref[:, gcols] * xbc_refs[g][rows, 0:GROUP_WIDTH]
        yg = y * _silu(proj_refs[g][halo + c * q:halo + (c + 1) * q, 0:GROUP_WIDTH])
        yn = yg * lax.rsqrt(jnp.mean(yg * yg, axis=-1, keepdims=True) + EPS) * norm_w_ref[:, gcols]
        y_ref[rows, gcols] = yn.astype(BF16)

    units = [(g, c) for g in range(SSD_GROUPS) for c in range(n_chunks)]
    project(0)
    conv(0)
    exg = expand_heads(0)
    pending = stage_cb(0, 0, exg)
    for n, (g, c) in enumerate(units):
        cur = pending
        if c == 0 and g + 1 < SSD_GROUPS:
            project(g + 1)
        stage_state(g, c, cur)
        if n + 1 < len(units):
            g1, c1 = units[n + 1]
            if c1 == 0:
                conv(g1)
                exg = expand_heads(g1)
            pending = stage_cb(g1, c1, exg)
        stage_out(g, c, cur)

    out_ref[...] = x + _rms(_out_proj(y_ref[...], w_out_ref), post_g_ref[...])


def _group_major(parts, widths):
    lead = parts[0].shape[:-1]
    pieces = [p.reshape(*lead, SSD_GROUPS, w) for p, w in zip(parts, widths)]
    return jnp.concatenate(pieces, axis=-1).reshape(*lead, SSD_GROUPS * sum(widths))


def _ssd_layer(h, pre_g, post_g, w_in, conv_w, conv_b, dt_bias, a_log, d_skip, norm_w, w_out,
               *, seq_len):
    t = h.shape[0]
    tm, q = SSD_TOKEN_TILE, SSD_CHUNK
    assert t % tm == 0 and seq_len % tm == 0 and tm % q == 0
    n_main = D_INNER + SSD_XBC
    xbc_widths = (GROUP_WIDTH, SSD_STATE, SSD_STATE)
    split_xbc = lambda v: (v[..., :D_INNER], v[..., D_INNER:D_INNER + SSD_GN], v[..., D_INNER + SSD_GN:])
    w_main = _group_major((w_in[:, :D_INNER],) + split_xbc(w_in[:, D_INNER:n_main]),
                          (GROUP_WIDTH,) + xbc_widths)
    conv_w_g = _group_major(split_xbc(conv_w), xbc_widths)
    conv_b_g = _group_major(split_xbc(conv_b[None]), xbc_widths)
    pad_heads = lambda v: jnp.pad(v, (0, LANES - SSD_HEADS))[None]
    w_dt = jnp.pad(w_in[:, n_main:], ((0, 0), (0, LANES - SSD_HEADS))).astype(BF16)
    piece_head = jnp.arange(LANES) % SSD_HEADS
    valid = jnp.arange(LANES) < 3 * SSD_HEADS
    chan_head = jnp.arange(D_INNER) // SSD_HEAD_DIM
    expand = ((piece_head[:, None] == chan_head[None, :]) & valid[:, None]).astype(BF16)

    resident, resident3 = _resident_specs()
    return pl.pallas_call(
        functools.partial(_ssd_layer_kernel, tm=tm, q=q, tiles_per_seq=seq_len // tm),
        grid=(t // tm,),
        in_specs=[
            pl.BlockSpec((tm, D_MODEL), lambda i: (i, 0)),
            resident((1, D_MODEL)),
            resident3((SSD_GROUPS, D_MODEL, GROUP_PROJ)),
            resident((D_MODEL, LANES)),
            resident((SSD_CONV, SSD_XBC)),
            resident((1, SSD_XBC)),
            resident((1, LANES)),
            resident((1, LANES)),
            resident((1, D_INNER)),
            resident((1, D_INNER)),
            resident3((D_MODEL // OUT_COL_BLOCK, D_INNER, OUT_COL_BLOCK)),
            resident((1, D_MODEL)),
            resident((LANES, D_INNER)),
        ],
        out_specs=pl.BlockSpec((tm, D_MODEL), lambda i: (i, 0)),
        out_shape=jax.ShapeDtypeStruct((t, D_MODEL), F32),
        scratch_shapes=[
            *[pltpu.VMEM((SUBLANES + tm, GROUP_PROJ), F32) for _ in range(SSD_GROUPS)],
            *[pltpu.VMEM((tm, GROUP_XBC), F32) for _ in range(SSD_GROUPS)],
            pltpu.VMEM((SSD_GROUPS, SSD_STATE, GROUP_WIDTH), F32),
            pltpu.VMEM((tm, D_INNER), BF16),
        ],
        compiler_params=pltpu.CompilerParams(
            dimension_semantics=("arbitrary",), vmem_limit_bytes=VMEM_LIMIT_BYTES),
        name="ssd_layer",
    )(h, pre_g[None], _col_blocks(w_main, GROUP_PROJ), w_dt, conv_w_g, conv_b_g, pad_heads(dt_bias),
      pad_heads(a_log), jnp.repeat(d_skip, SSD_HEAD_DIM)[None], norm_w[None],
      _col_blocks(w_out, OUT_COL_BLOCK), post_g[None], expand)


def kernel(x, pre_norm, post_norm, sc_w_in, sc_conv_w, sc_w_out, ssd_w_in, ssd_conv_w, ssd_conv_b,
           ssd_dt_bias, ssd_a_log, ssd_d_skip, ssd_norm, ssd_w_out):
    bsz, seq_len, _ = x.shape
    h = x.reshape(bsz * seq_len, D_MODEL)
    for i in range(pre_norm.shape[0]):
        j = i // N_MIXERS
        if i % N_MIXERS == 0:
            h = _sc_layer(h, pre_norm[i], post_norm[i], sc_w_in[j], sc_conv_w[j], sc_w_out[j],
                          seq_len=seq_len)
        else:
            h = _ssd_layer(h, pre_norm[i], post_norm[i], ssd_w_in[j], ssd_conv_w[j], ssd_conv_b[j],
                           ssd_dt_bias[j], ssd_a_log[j], ssd_d_skip[j], ssd_norm[j], ssd_w_out[j],
                           seq_len=seq_len)
    return h.reshape(bsz, seq_len, D_MODEL)
```

```python
import functools

import jax
import jax.numpy as jnp
from jax import lax
from jax.experimental import pallas as pl
from jax.experimental.pallas import tpu as pltpu

D_MODEL = 1024
D_INNER = 2048
N_MIXERS = 2
SSD_HEAD_DIM = 64
SSD_HEADS = D_INNER // SSD_HEAD_DIM
SSD_GROUPS = 8
SSD_HPG = SSD_HEADS // SSD_GROUPS
SSD_STATE = 128
SSD_GN = SSD_GROUPS * SSD_STATE
SSD_XBC = D_INNER + 2 * SSD_GN
SSD_CONV = 4
GROUP_WIDTH = D_INNER // SSD_GROUPS
GROUP_XBC = GROUP_WIDTH + 2 * SSD_STATE
GROUP_PROJ = GROUP_WIDTH + GROUP_XBC
EPS = 1e-6

LANES = 128
SUBLANES = 8
VMEM_LIMIT_BYTES = 52 * 1024 * 1024

SC_TOKEN_TILE = 512
SC_COL_TILE = 512
SSD_TOKEN_TILE = 256
SSD_CHUNK = 128
OUT_COL_BLOCK = 512

F32 = jnp.float32
BF16 = jnp.bfloat16


def _rms(x, g):
    return x * lax.rsqrt(jnp.mean(x * x, axis=-1, keepdims=True) + EPS) * g


def _silu(x):
    return x * jax.nn.sigmoid(x)


def _dot(a, b):
    return jnp.dot(a, b, preferred_element_type=F32)


def _resident(shape):
    return pl.BlockSpec(shape, lambda i: (0, 0), pipeline_mode=pl.Buffered(1))


def _col_block_specs(k, n, width):
    return [pl.BlockSpec((k, width), lambda i, b=b: (0, b), pipeline_mode=pl.Buffered(1))
            for b in range(n // width)]


def _out_proj(y, w_out_refs):
    return jnp.concatenate([_dot(y, w_ref[...]) for w_ref in w_out_refs], axis=1)


def _split3(v):
    hi = v.astype(BF16).astype(F32)
    r = v - hi
    mid = r.astype(BF16).astype(F32)
    lo = (r - mid).astype(BF16).astype(F32)
    return hi, mid, lo


def _sc_layer_kernel(*refs, tm, tn, tiles_per_seq):
    refs = iter(refs)
    h_ref, pre_g_ref = next(refs), next(refs)
    w_in_refs = [next(refs) for _ in range(4 * D_INNER // tn)]
    conv_w_ref = next(refs)
    w_out_refs = [next(refs) for _ in range(D_MODEL // OUT_COL_BLOCK)]
    post_g_ref, out_ref, halo_ref, y_ref = refs
    i = pl.program_id(0)

    @pl.when(i % tiles_per_seq == 0)
    def _():
        halo_ref[...] = jnp.zeros_like(halo_ref)

    x = h_ref[...]
    u = _rms(x, pre_g_ref[...]).astype(BF16)
    for j in range(D_INNER // tn):
        c0 = j * tn

        def proj(k):
            return _dot(u, w_in_refs[k * (D_INNER // tn) + j][...])

        cv = proj(2) * proj(3)
        ext = jnp.concatenate([halo_ref[:, c0:c0 + tn], cv], axis=0)
        halo_ref[:, c0:c0 + tn] = cv[tm - SUBLANES:, :]
        cw = conv_w_ref[:, c0:c0 + tn]
        conv = (cw[2:3] * cv
                + cw[1:2] * pltpu.roll(ext, 1, 0)[SUBLANES:]
                + cw[0:1] * pltpu.roll(ext, 2, 0)[SUBLANES:])
        y = proj(1) * conv * _silu(proj(0))
        y_ref[:, c0:c0 + tn] = y.astype(BF16)
    out_ref[...] = x + _rms(_out_proj(y_ref[...], w_out_refs), post_g_ref[...])


def _sc_layer(h, pre_g, post_g, w_in, conv_w, w_out, *, seq_len):
    t = h.shape[0]
    tm, tn = SC_TOKEN_TILE, SC_COL_TILE
    assert t % tm == 0 and seq_len % tm == 0 and D_INNER % tn == 0
    w_in_specs = _col_block_specs(D_MODEL, 4 * D_INNER, tn)
    w_out_specs = _col_block_specs(D_INNER, D_MODEL, OUT_COL_BLOCK)
    w_in16, w_out16 = w_in.astype(BF16), w_out.astype(BF16)
    return pl.pallas_call(
        functools.partial(_sc_layer_kernel, tm=tm, tn=tn, tiles_per_seq=seq_len // tm),
        grid=(t // tm,),
        in_specs=[
            pl.BlockSpec((tm, D_MODEL), lambda i: (i, 0)),
            _resident((1, D_MODEL)),
            *w_in_specs,
            _resident(conv_w.shape),
            *w_out_specs,
            _resident((1, D_MODEL)),
        ],
        out_specs=pl.BlockSpec((tm, D_MODEL), lambda i: (i, 0)),
        out_shape=jax.ShapeDtypeStruct((t, D_MODEL), F32),
        scratch_shapes=[
            pltpu.VMEM((SUBLANES, D_INNER), F32),
            pltpu.VMEM((tm, D_INNER), BF16),
        ],
        compiler_params=pltpu.CompilerParams(
            dimension_semantics=("arbitrary",), vmem_limit_bytes=VMEM_LIMIT_BYTES),
        name="short_conv_layer",
    )(h, pre_g[None], *[w_in16] * len(w_in_specs), conv_w, *[w_out16] * len(w_out_specs),
      post_g[None])


def _ssd_layer_kernel(*refs, tm, q, tiles_per_seq):
    refs = iter(refs)
    h_ref, pre_g_ref = next(refs), next(refs)
    w_main_refs = [next(refs) for _ in range(SSD_GROUPS)]
    w_dt_ref, conv_w_ref, conv_b_ref, dt_bias_ref, a_log_ref, dskip_ref, norm_w_ref = (
        next(refs) for _ in range(7))
    w_out_refs = [next(refs) for _ in range(D_MODEL // OUT_COL_BLOCK)]
    post_g_ref, expand_ref, out_ref = next(refs), next(refs), next(refs)
    proj_refs = [next(refs) for _ in range(SSD_GROUPS)]
    xbc_refs = [next(refs) for _ in range(SSD_GROUPS)]
    state_ref, y_ref = refs
    i = pl.program_id(0)
    halo = SUBLANES
    n_chunks = tm // q

    @pl.when(i % tiles_per_seq == 0)
    def _():
        state_ref[...] = jnp.zeros_like(state_ref)
        for proj_ref in proj_refs:
            proj_ref[:, 0:halo, :] = jnp.zeros((GROUP_PROJ // LANES, halo, LANES), F32)

    x = h_ref[...]
    u = _rms(x, pre_g_ref[...]).astype(BF16)
    lane = lax.broadcasted_iota(jnp.int32, (1, LANES), 1)
    head_lane = lane < SSD_HEADS
    dt = jnp.where(head_lane, jax.nn.softplus(_dot(u, w_dt_ref[...]) + dt_bias_ref[...]), 0.0)
    a = -jnp.exp(a_log_ref[...])

    row_id = lax.broadcasted_iota(jnp.int32, (q, q), 0)
    col_id = lax.broadcasted_iota(jnp.int32, (q, q), 1)
    tril = row_id >= col_id
    tril_bf = tril.astype(BF16)
    bd_row = lax.broadcasted_iota(jnp.int32, (SSD_HPG * q, GROUP_WIDTH), 0) // q
    bd_col = lax.broadcasted_iota(jnp.int32, (SSD_HPG * q, GROUP_WIDTH), 1) // SSD_HEAD_DIM
    block_diag = bd_row == bd_col

    def pack3(v):
        hi, mid, lo = _split3(jnp.where(head_lane, v, 0.0))
        return (hi + pltpu.roll(mid, SSD_HEADS, 1) + pltpu.roll(lo, 2 * SSD_HEADS, 1)).astype(BF16)

    acums, acum_ts, dt_ts, packs = [], [], [], []
    for c in range(n_chunks):
        dtc = dt[c * q:(c + 1) * q]
        hi, mid, lo = _split3(dtc * a)
        cs = _dot(tril_bf, jnp.concatenate([hi, mid, lo], axis=1).astype(BF16))
        acum = cs[:, :LANES] + cs[:, LANES:2 * LANES] + cs[:, 2 * LANES:]
        w1 = dtc * jnp.exp(acum[q - 1:q, :] - acum)
        acums.append(acum)
        acum_ts.append(acum.T)
        dt_ts.append(dtc.T)
        packs += [pack3(w1), pack3(jnp.exp(acum))]
    packed = jnp.concatenate(packs, axis=0)

    def project(g):
        p = _dot(u, w_main_refs[g][...])
        for j in range(GROUP_PROJ // LANES):
            proj_refs[g][j, halo:, :] = p[:, j * LANES:(j + 1) * LANES]

    def conv(g):
        proj_ref = proj_refs[g]
        for j in range(GROUP_XBC // LANES):
            jt = GROUP_WIDTH // LANES + j
            ccols = slice(g * GROUP_XBC + j * LANES, g * GROUP_XBC + (j + 1) * LANES)
            w = conv_w_ref[:, ccols]
            acc = conv_b_ref[:, ccols]
            for k in range(SSD_CONV):
                acc = acc + w[SSD_CONV - 1 - k:SSD_CONV - k] * proj_ref[jt, pl.ds(halo - k, tm), :]
            proj_ref[jt, 0:halo, :] = proj_ref[jt, tm:tm + halo, :]
            xbc_refs[g][:, j * LANES:(j + 1) * LANES] = _silu(acc)

    def expand_heads(g):
        return _dot(packed, expand_ref[:, g * GROUP_WIDTH:(g + 1) * GROUP_WIDTH])

    def stage_cb(g, c, exg):
        rows = slice(c * q, (c + 1) * q)
        xbc_ref = xbc_refs[g]
        xg = xbc_ref[rows, 0:GROUP_WIDTH]
        bg = xbc_ref[rows, GROUP_WIDTH:GROUP_WIDTH + SSD_STATE]
        cb16 = xbc_ref[rows, GROUP_WIDTH + SSD_STATE:GROUP_XBC].astype(BF16)
        e0 = c * 2 * q
        w1x, eax = exg[e0:e0 + q], exg[e0 + q:e0 + 2 * q]
        cb = lax.dot_general(cb16, bg.astype(BF16), (((1,), (1,)), ((), ())),
                             preferred_element_type=F32)
        return dict(cb=cb, cb16=cb16, bt=bg.T.astype(BF16), x16=xg.astype(BF16),
                    xw=(xg * w1x).astype(BF16), eax=eax)

    def stage_state(g, c, v):
        hg = state_ref[g]
        v["y_off"] = _dot(v["cb16"], hg.astype(BF16)) * v["eax"]
        state_ref[g] = hg * v["eax"][q - 1:q, :] + _dot(v["bt"], v["xw"])

    def stage_out(g, c, v):
        rows = slice(c * q, (c + 1) * q)
        gcols = slice(g * GROUP_WIDTH, (g + 1) * GROUP_WIDTH)
        lmats = []
        for k in range(SSD_HPG):
            hh = g * SSD_HPG + k
            seg = acums[c][:, hh:hh + 1] - acum_ts[c][hh:hh + 1, :]
            decay_dt = jnp.exp(jnp.where(tril, seg, -jnp.inf)) * dt_ts[c][hh:hh + 1, :]
            lmats.append((v["cb"] * decay_dt).astype(BF16))
        lcat = jnp.concatenate(lmats, axis=1)
        xbd = jnp.where(block_diag, jnp.concatenate([v["x16"]] * SSD_HPG, axis=0), 0)
        y = _dot(lcat, xbd) + v["y_off"] + dskip_ref[:, gcols] * xbc_refs[g][rows, 0:GROUP_WIDTH]
        zrows = slice(halo + c * q, halo + (c + 1) * q)
        zg = jnp.concatenate([proj_refs[g][j, zrows, :] for j in range(GROUP_WIDTH // LANES)], axis=1)
        yg = y * _silu(zg)
        yn = yg * lax.rsqrt(jnp.mean(yg * yg, axis=-1, keepdims=True) + EPS) * norm_w_ref[:, gcols]
        y_ref[rows, gcols] = yn.astype(BF16)

    units = [(g, c) for g in range(SSD_GROUPS) for c in range(n_chunks)]
    ahead = 2
    proj_ahead = 4
    exgs = {}
    pending = []

    def start_unit(n):
        g, c = units[n]
        if c == 0:
            conv(g)
            exgs[g] = expand_heads(g)
        pending.append(stage_cb(g, c, exgs[g]))

    for n in range(proj_ahead):
        if units[n][1] == 0:
            project(units[n][0])
    for n in range(ahead):
        start_unit(n)
    for n, (g, c) in enumerate(units):
        cur = pending.pop(0)
        if n + proj_ahead < len(units) and units[n + proj_ahead][1] == 0:
            project(units[n + proj_ahead][0])
        stage_state(g, c, cur)
        if n + ahead < len(units):
            start_unit(n + ahead)
        stage_out(g, c, cur)

    out_ref[...] = x + _rms(_out_proj(y_ref[...], w_out_refs), post_g_ref[...])


def _group_major(parts, widths):
    lead = parts[0].shape[:-1]
    pieces = [p.reshape(*lead, SSD_GROUPS, w) for p, w in zip(parts, widths)]
    return jnp.concatenate(pieces, axis=-1).reshape(*lead, SSD_GROUPS * sum(widths))


def _ssd_layer(h, pre_g, post_g, w_in, conv_w, conv_b, dt_bias, a_log, d_skip, norm_w, w_out,
               *, seq_len):
    t = h.shape[0]
    tm, q = SSD_TOKEN_TILE, SSD_CHUNK
    assert t % tm == 0 and seq_len % tm == 0 and tm % q == 0
    n_main = D_INNER + SSD_XBC
    xbc_widths = (GROUP_WIDTH, SSD_STATE, SSD_STATE)
    split_xbc = lambda v: (v[..., :D_INNER], v[..., D_INNER:D_INNER + SSD_GN], v[..., D_INNER + SSD_GN:])
    w_main = _group_major((w_in[:, :D_INNER],) + split_xbc(w_in[:, D_INNER:n_main]),
                          (GROUP_WIDTH,) + xbc_widths).astype(BF16)
    conv_w_g = _group_major(split_xbc(conv_w), xbc_widths)
    conv_b_g = _group_major(split_xbc(conv_b[None]), xbc_widths)
    pad_heads = lambda v: jnp.pad(v, (0, LANES - SSD_HEADS))[None]
    w_dt = jnp.pad(w_in[:, n_main:], ((0, 0), (0, LANES - SSD_HEADS))).astype(BF16)
    piece_head = jnp.arange(LANES) % SSD_HEADS
    valid = jnp.arange(LANES) < 3 * SSD_HEADS
    chan_head = jnp.arange(D_INNER) // SSD_HEAD_DIM
    expand = ((piece_head[:, None] == chan_head[None, :]) & valid[:, None]).astype(BF16)

    w_main_specs = _col_block_specs(D_MODEL, n_main, GROUP_PROJ)
    w_out_specs = _col_block_specs(D_INNER, D_MODEL, OUT_COL_BLOCK)
    w_out16 = w_out.astype(BF16)
    return pl.pallas_call(
        functools.partial(_ssd_layer_kernel, tm=tm, q=q, tiles_per_seq=seq_len // tm),
        grid=(t // tm,),
        in_specs=[
            pl.BlockSpec((tm, D_MODEL), lambda i: (i, 0)),
            _resident((1, D_MODEL)),
            *w_main_specs,
            _resident((D_MODEL, LANES)),
            _resident((SSD_CONV, SSD_XBC)),
            _resident((1, SSD_XBC)),
            _resident((1, LANES)),
            _resident((1, LANES)),
            _resident((1, D_INNER)),
            _resident((1, D_INNER)),
            *w_out_specs,
            _resident((1, D_MODEL)),
            _resident((LANES, D_INNER)),
        ],
        out_specs=pl.BlockSpec((tm, D_MODEL), lambda i: (i, 0)),
        out_shape=jax.ShapeDtypeStruct((t, D_MODEL), F32),
        scratch_shapes=[
            *[pltpu.VMEM((GROUP_PROJ // LANES, SUBLANES + tm, LANES), F32) for _ in range(SSD_GROUPS)],
            *[pltpu.VMEM((tm, GROUP_XBC), F32) for _ in range(SSD_GROUPS)],
            pltpu.VMEM((SSD_GROUPS, SSD_STATE, GROUP_WIDTH), F32),
            pltpu.VMEM((tm, D_INNER), BF16),
        ],
        compiler_params=pltpu.CompilerParams(
            dimension_semantics=("arbitrary",), vmem_limit_bytes=VMEM_LIMIT_BYTES),
        name="ssd_layer",
    )(h, pre_g[None], *[w_main] * len(w_main_specs), w_dt, conv_w_g, conv_b_g, pad_heads(dt_bias),
      pad_heads(a_log), jnp.repeat(d_skip, SSD_HEAD_DIM)[None], norm_w[None],
      *[w_out16] * len(w_out_specs), post_g[None], expand)


def kernel(x, pre_norm, post_norm, sc_w_in, sc_conv_w, sc_w_out, ssd_w_in, ssd_conv_w, ssd_conv_b,
           ssd_dt_bias, ssd_a_log, ssd_d_skip, ssd_norm, ssd_w_out):
    bsz, seq_len, _ = x.shape
    h = x.reshape(bsz * seq_len, D_MODEL)
    for i in range(pre_norm.shape[0]):
        j = i // N_MIXERS
        if i % N_MIXERS == 0:
            h = _sc_layer(h, pre_norm[i], post_norm[i], sc_w_in[j], sc_conv_w[j], sc_w_out[j],
                          seq_len=seq_len)
        else:
            h = _ssd_layer(h, pre_norm[i], post_norm[i], ssd_w_in[j], ssd_conv_w[j], ssd_conv_b[j],
                           ssd_dt_bias[j], ssd_a_log[j], ssd_d_skip[j], ssd_norm[j], ssd_w_out[j],
                           seq_len=seq_len)
    return h.reshape(bsz, seq_len, D_MODEL)
```

```python
import functools

import jax
import jax.numpy as jnp
from jax import lax
from jax.experimental import pallas as pl
from jax.experimental.pallas import tpu as pltpu

D_MODEL = 1024
D_INNER = 2048
N_MIXERS = 2
SSD_HEAD_DIM = 64
SSD_HEADS = D_INNER // SSD_HEAD_DIM
SSD_GROUPS = 8
SSD_HPG = SSD_HEADS // SSD_GROUPS
SSD_STATE = 128
SSD_GN = SSD_GROUPS * SSD_STATE
SSD_XBC = D_INNER + 2 * SSD_GN
SSD_CONV = 4
GROUP_WIDTH = D_INNER // SSD_GROUPS
GROUP_XBC = GROUP_WIDTH + 2 * SSD_STATE
GROUP_PROJ = GROUP_WIDTH + GROUP_XBC
EPS = 1e-6

LANES = 128
SUBLANES = 8
VMEM_LIMIT_BYTES = 52 * 1024 * 1024

SC_TOKEN_TILE = 512
SC_COL_TILE = 512
SSD_TOKEN_TILE = 512
SSD_CHUNK = 128
OUT_COL_BLOCK = 512

F32 = jnp.float32
BF16 = jnp.bfloat16


def _rms(x, g):
    return x * lax.rsqrt(jnp.mean(x * x, axis=-1, keepdims=True) + EPS) * g


def _silu(x):
    return x * jax.nn.sigmoid(x)


def _dot(a, b):
    return jnp.dot(a, b, preferred_element_type=F32)


def _resident(shape):
    return pl.BlockSpec(shape, lambda i: (0, 0), pipeline_mode=pl.Buffered(1))


def _col_block_specs(k, n, width):
    return [pl.BlockSpec((k, width), lambda i, b=b: (0, b), pipeline_mode=pl.Buffered(1))
            for b in range(n // width)]


def _out_proj(y, w_out_refs):
    return jnp.concatenate([_dot(y, w_ref[...]) for w_ref in w_out_refs], axis=1)


def _split3(v):
    hi = v.astype(BF16).astype(F32)
    r = v - hi
    mid = r.astype(BF16).astype(F32)
    lo = (r - mid).astype(BF16).astype(F32)
    return hi, mid, lo


def _sc_layer_kernel(*refs, tm, tn, tiles_per_seq):
    refs = iter(refs)
    h_ref, pre_g_ref = next(refs), next(refs)
    w_in_refs = [next(refs) for _ in range(4 * D_INNER // tn)]
    conv_w_ref = next(refs)
    w_out_refs = [next(refs) for _ in range(D_MODEL // OUT_COL_BLOCK)]
    post_g_ref, out_ref, halo_ref, y_ref = refs
    i = pl.program_id(0)

    @pl.when(i % tiles_per_seq == 0)
    def _():
        halo_ref[...] = jnp.zeros_like(halo_ref)

    x = h_ref[...]
    u = _rms(x, pre_g_ref[...]).astype(BF16)
    for j in range(D_INNER // tn):
        c0 = j * tn

        def proj(k):
            return _dot(u, w_in_refs[k * (D_INNER // tn) + j][...])

        cv = proj(2) * proj(3)
        ext = jnp.concatenate([halo_ref[:, c0:c0 + tn], cv], axis=0)
        halo_ref[:, c0:c0 + tn] = cv[tm - SUBLANES:, :]
        cw = conv_w_ref[:, c0:c0 + tn]
        conv = (cw[2:3] * cv
                + cw[1:2] * pltpu.roll(ext, 1, 0)[SUBLANES:]
                + cw[0:1] * pltpu.roll(ext, 2, 0)[SUBLANES:])
        y = proj(1) * conv * _silu(proj(0))
        y_ref[:, c0:c0 + tn] = y.astype(BF16)
    out_ref[...] = x + _rms(_out_proj(y_ref[...], w_out_refs), post_g_ref[...])


def _sc_layer(h, pre_g, post_g, w_in, conv_w, w_out, *, seq_len):
    t = h.shape[0]
    tm, tn = SC_TOKEN_TILE, SC_COL_TILE
    assert t % tm == 0 and seq_len % tm == 0 and D_INNER % tn == 0
    w_in_specs = _col_block_specs(D_MODEL, 4 * D_INNER, tn)
    w_out_specs = _col_block_specs(D_INNER, D_MODEL, OUT_COL_BLOCK)
    w_in16, w_out16 = w_in.astype(BF16), w_out.astype(BF16)
    return pl.pallas_call(
        functools.partial(_sc_layer_kernel, tm=tm, tn=tn, tiles_per_seq=seq_len // tm),
        grid=(t // tm,),
        in_specs=[
            pl.BlockSpec((tm, D_MODEL), lambda i: (i, 0)),
            _resident((1, D_MODEL)),
            *w_in_specs,
            _resident(conv_w.shape),
            *w_out_specs,
            _resident((1, D_MODEL)),
        ],
        out_specs=pl.BlockSpec((tm, D_MODEL), lambda i: (i, 0)),
        out_shape=jax.ShapeDtypeStruct((t, D_MODEL), F32),
        scratch_shapes=[
            pltpu.VMEM((SUBLANES, D_INNER), F32),
            pltpu.VMEM((tm, D_INNER), BF16),
        ],
        compiler_params=pltpu.CompilerParams(
            dimension_semantics=("arbitrary",), vmem_limit_bytes=VMEM_LIMIT_BYTES),
        name="short_conv_layer",
    )(h, pre_g[None], *[w_in16] * len(w_in_specs), conv_w, *[w_out16] * len(w_out_specs),
      post_g[None])


def _ssd_layer_kernel(*refs, tm, q, tiles_per_seq):
    refs = iter(refs)
    h_ref, pre_g_ref = next(refs), next(refs)
    w_main_refs = [next(refs) for _ in range(SSD_GROUPS)]
    w_dt_ref, conv_w_ref, conv_b_ref, dt_bias_ref, a_log_ref, dskip_ref, norm_w_ref = (
        next(refs) for _ in range(7))
    w_out_refs = [next(refs) for _ in range(D_MODEL // OUT_COL_BLOCK)]
    post_g_ref, out_ref = next(refs), next(refs)
    proj_refs = [next(refs) for _ in range(SSD_GROUPS)]
    xbc_refs = [next(refs) for _ in range(SSD_GROUPS)]
    state_ref, y_ref = refs
    i = pl.program_id(0)
    halo = SUBLANES
    n_chunks = tm // q

    @pl.when(i % tiles_per_seq == 0)
    def _():
        state_ref[...] = jnp.zeros_like(state_ref)
        for proj_ref in proj_refs:
            proj_ref[:, 0:halo, :] = jnp.zeros((GROUP_PROJ // LANES, halo, LANES), F32)

    x = h_ref[...]
    u = _rms(x, pre_g_ref[...]).astype(BF16)
    lane = lax.broadcasted_iota(jnp.int32, (1, LANES), 1)
    head_lane = lane < SSD_HEADS
    dt = jnp.where(head_lane, jax.nn.softplus(_dot(u, w_dt_ref[...]) + dt_bias_ref[...]), 0.0)
    a = -jnp.exp(a_log_ref[...])

    row_id = lax.broadcasted_iota(jnp.int32, (q, q), 0)
    col_id = lax.broadcasted_iota(jnp.int32, (q, q), 1)
    tril = row_id >= col_id
    tril_bf = tril.astype(BF16)
    bd_row = lax.broadcasted_iota(jnp.int32, (SSD_HPG * q, GROUP_WIDTH), 0) // q
    bd_col = lax.broadcasted_iota(jnp.int32, (SSD_HPG * q, GROUP_WIDTH), 1) // SSD_HEAD_DIM
    block_diag = bd_row == bd_col
    first_head_lanes = lane < SSD_HEAD_DIM

    acums, acum_ts, dt_ts, w1s = [], [], [], []
    for c in range(n_chunks):
        dtc = dt[c * q:(c + 1) * q]
        hi, mid, lo = _split3(dtc * a)
        cs = _dot(tril_bf, jnp.concatenate([hi, mid, lo], axis=1).astype(BF16))
        acum = cs[:, :LANES] + cs[:, LANES:2 * LANES] + cs[:, 2 * LANES:]
        acums.append(acum)
        acum_ts.append(acum.T)
        dt_ts.append(dtc.T)
        w1s.append(dtc * jnp.exp(acum[q - 1:q, :] - acum))

    def project(g):
        p = _dot(u, w_main_refs[g][...])
        for j in range(GROUP_PROJ // LANES):
            proj_refs[g][j, halo:, :] = p[:, j * LANES:(j + 1) * LANES]

    def conv(g):
        proj_ref = proj_refs[g]
        for j in range(GROUP_XBC // LANES):
            jt = GROUP_WIDTH // LANES + j
            ccols = slice(g * GROUP_XBC + j * LANES, g * GROUP_XBC + (j + 1) * LANES)
            w = conv_w_ref[:, ccols]
            acc = conv_b_ref[:, ccols]
            for k in range(SSD_CONV):
                acc = acc + w[SSD_CONV - 1 - k:SSD_CONV - k] * proj_ref[jt, pl.ds(halo - k, tm), :]
            proj_ref[jt, 0:halo, :] = proj_ref[jt, tm:tm + halo, :]
            xbc_refs[g][:, j * LANES:(j + 1) * LANES] = _silu(acc)

    def per_channel(cols):
        pairs = [jnp.where(first_head_lanes, cols[k], cols[k + 1]) for k in range(0, SSD_HPG, 2)]
        return jnp.concatenate(pairs, axis=1)

    def stage_cb(g, c):
        rows = slice(c * q, (c + 1) * q)
        xbc_ref = xbc_refs[g]
        xg = xbc_ref[rows, 0:GROUP_WIDTH]
        bg = xbc_ref[rows, GROUP_WIDTH:GROUP_WIDTH + SSD_STATE]
        cb16 = xbc_ref[rows, GROUP_WIDTH + SSD_STATE:GROUP_XBC].astype(BF16)
        heads = range(g * SSD_HPG, (g + 1) * SSD_HPG)
        acols = [jnp.broadcast_to(acums[c][:, h:h + 1], (q, LANES)) for h in heads]
        w1x = per_channel([jnp.broadcast_to(w1s[c][:, h:h + 1], (q, LANES)) for h in heads])
        eax = per_channel([jnp.exp(col) for col in acols])
        cb = lax.dot_general(cb16, bg.astype(BF16), (((1,), (1,)), ((), ())),
                             preferred_element_type=F32)
        return dict(cb=cb, cb16=cb16, bt=bg.T.astype(BF16), x16=xg.astype(BF16),
                    xw=(xg * w1x).astype(BF16), eax=eax)

    def stage_state(g, c, v):
        hg = state_ref[g]
        v["y_off"] = _dot(v["cb16"], hg.astype(BF16)) * v["eax"]
        state_ref[g] = hg * v["eax"][q - 1:q, :] + _dot(v["bt"], v["xw"])

    def stage_out(g, c, v):
        rows = slice(c * q, (c + 1) * q)
        gcols = slice(g * GROUP_WIDTH, (g + 1) * GROUP_WIDTH)
        lmats = []
        for k in range(SSD_HPG):
            hh = g * SSD_HPG + k
            seg = acums[c][:, hh:hh + 1] - acum_ts[c][hh:hh + 1, :]
            decay_dt = jnp.exp(jnp.where(tril, seg, -jnp.inf)) * dt_ts[c][hh:hh + 1, :]
            lmats.append((v["cb"] * decay_dt).astype(BF16))
        lcat = jnp.concatenate(lmats, axis=1)
        xbd = jnp.where(block_diag, jnp.concatenate([v["x16"]] * SSD_HPG, axis=0), 0)
        y = _dot(lcat, xbd) + v["y_off"] + dskip_ref[:, gcols] * xbc_refs[g][rows, 0:GROUP_WIDTH]
        zrows = slice(halo + c * q, halo + (c + 1) * q)
        zg = jnp.concatenate([proj_refs[g][j, zrows, :] for j in range(GROUP_WIDTH // LANES)], axis=1)
        yg = y * _silu(zg)
        yn = yg * lax.rsqrt(jnp.mean(yg * yg, axis=-1, keepdims=True) + EPS) * norm_w_ref[:, gcols]
        y_ref[rows, gcols] = yn.astype(BF16)

    units = [(g, c) for g in range(SSD_GROUPS) for c in range(n_chunks)]
    ahead = 1
    proj_ahead = 3
    pending = []

    def start_unit(n):
        g, c = units[n]
        if c == 0:
            conv(g)
        pending.append(stage_cb(g, c))

    for n in range(proj_ahead):
        if units[n][1] == 0:
            project(units[n][0])
    for n in range(ahead):
        start_unit(n)
    for n, (g, c) in enumerate(units):
        cur = pending.pop(0)
        if n + proj_ahead < len(units) and units[n + proj_ahead][1] == 0:
            project(units[n + proj_ahead][0])
        stage_state(g, c, cur)
        if n + ahead < len(units):
            start_unit(n + ahead)
        stage_out(g, c, cur)

    out_ref[...] = x + _rms(_out_proj(y_ref[...], w_out_refs), post_g_ref[...])


def _group_major(parts, widths):
    lead = parts[0].shape[:-1]
    pieces = [p.reshape(*lead, SSD_GROUPS, w) for p, w in zip(parts, widths)]
    return jnp.concatenate(pieces, axis=-1).reshape(*lead, SSD_GROUPS * sum(widths))


def _ssd_layer(h, pre_g, post_g, w_in, conv_w, conv_b, dt_bias, a_log, d_skip, norm_w, w_out,
               *, seq_len):
    t = h.shape[0]
    tm, q = SSD_TOKEN_TILE, SSD_CHUNK
    assert t % tm == 0 and seq_len % tm == 0 and tm % q == 0
    n_main = D_INNER + SSD_XBC
    xbc_widths = (GROUP_WIDTH, SSD_STATE, SSD_STATE)
    split_xbc = lambda v: (v[..., :D_INNER], v[..., D_INNER:D_INNER + SSD_GN], v[..., D_INNER + SSD_GN:])
    w_main = _group_major((w_in[:, :D_INNER],) + split_xbc(w_in[:, D_INNER:n_main]),
                          (GROUP_WIDTH,) + xbc_widths).astype(BF16)
    conv_w_g = _group_major(split_xbc(conv_w), xbc_widths)
    conv_b_g = _group_major(split_xbc(conv_b[None]), xbc_widths)
    pad_heads = lambda v: jnp.pad(v, (0, LANES - SSD_HEADS))[None]
    w_dt = jnp.pad(w_in[:, n_main:], ((0, 0), (0, LANES - SSD_HEADS))).astype(BF16)

    w_main_specs = _col_block_specs(D_MODEL, n_main, GROUP_PROJ)
    w_out_specs = _col_block_specs(D_INNER, D_MODEL, OUT_COL_BLOCK)
    w_out16 = w_out.astype(BF16)
    return pl.pallas_call(
        functools.partial(_ssd_layer_kernel, tm=tm, q=q, tiles_per_seq=seq_len // tm),
        grid=(t // tm,),
        in_specs=[
            pl.BlockSpec((tm, D_MODEL), lambda i: (i, 0)),
            _resident((1, D_MODEL)),
            *w_main_specs,
            _resident((D_MODEL, LANES)),
            _resident((SSD_CONV, SSD_XBC)),
            _resident((1, SSD_XBC)),
            _resident((1, LANES)),
            _resident((1, LANES)),
            _resident((1, D_INNER)),
            _resident((1, D_INNER)),
            *w_out_specs,
            _resident((1, D_MODEL)),
        ],
        out_specs=pl.BlockSpec((tm, D_MODEL), lambda i: (i, 0)),
        out_shape=jax.ShapeDtypeStruct((t, D_MODEL), F32),
        scratch_shapes=[
            *[pltpu.VMEM((GROUP_PROJ // LANES, SUBLANES + tm, LANES), F32) for _ in range(SSD_GROUPS)],
            *[pltpu.VMEM((tm, GROUP_XBC), F32) for _ in range(SSD_GROUPS)],
            pltpu.VMEM((SSD_GROUPS, SSD_STATE, GROUP_WIDTH), F32),
            pltpu.VMEM((tm, D_INNER), BF16),
        ],
        compiler_params=pltpu.CompilerParams(
            dimension_semantics=("arbitrary",), vmem_limit_bytes=VMEM_LIMIT_BYTES),
        name="ssd_layer",
    )(h, pre_g[None], *[w_main] * len(w_main_specs), w_dt, conv_w_g, conv_b_g, pad_heads(dt_bias),
      pad_heads(a_log), jnp.repeat(d_skip, SSD_HEAD_DIM)[None], norm_w[None],
      *[w_out16] * len(w_out_specs), post_g[None])


def kernel(x, pre_norm, post_norm, sc_w_in, sc_conv_w, sc_w_out, ssd_w_in, ssd_conv_w, ssd_conv_b,
           ssd_dt_bias, ssd_a_log, ssd_d_skip, ssd_norm, ssd_w_out):
    bsz, seq_len, _ = x.shape
    h = x.reshape(bsz * seq_len, D_MODEL)
    for i in range(pre_norm.shape[0]):
        j = i // N_MIXERS
        if i % N_MIXERS == 0:
            h = _sc_layer(h, pre_norm[i], post_norm[i], sc_w_in[j], sc_conv_w[j], sc_w_out[j],
                          seq_len=seq_len)
        else:
            h = _ssd_layer(h, pre_norm[i], post_norm[i], ssd_w_in[j], ssd_conv_w[j], ssd_conv_b[j],
                           ssd_dt_bias[j], ssd_a_log[j], ssd_d_skip[j], ssd_norm[j], ssd_w_out[j],
                           seq_len=seq_len)
    return h.reshape(bsz, seq_len, D_MODEL)
```
